```python
import math
import jax, jax.numpy as jnp
from jax import lax
import numpy as np

D_MODEL = 1024
BATCH = 8
SEQ = 4096
DEPTH = 1

PLE_DIM = 256
D_RG = D_MODEL // 2
RG_BLOCKS = 8
RG_BLOCK = D_RG // RG_BLOCKS
CONV_WIDTH = 4
RG_C = 8.0
D_HG = D_MODEL // 2
HG_HEAD_DIM = 128
HG_HEADS = D_HG // HG_HEAD_DIM
HG_CHUNK = 64
D_MIX = D_RG + D_HG
D_IN = 2 * D_RG + 4 * D_HG
EPS = 1e-6

kernel_name = "hymba_style_rglru_hgrn2_block"


def rms_norm(x, w):
    xf = x.astype(jnp.float32)
    y = xf * lax.rsqrt(jnp.mean(xf * xf, axis=-1, keepdims=True) + EPS)
    return (y * w.astype(jnp.float32)).astype(x.dtype)


def causal_depthwise_conv(x, w, b):
    T = x.shape[1]
    xp = jnp.pad(x, ((0, 0), (CONV_WIDTH - 1, 0), (0, 0)))
    y = b
    for j in range(CONV_WIDTH):
        y = y + xp[:, j:j + T] * w[j]
    return y


def rg_lru(x, wa, ba, wx, bx, lam):
    B, T, _ = x.shape
    xf = x.astype(jnp.float32)
    xb = xf.reshape(B, T, RG_BLOCKS, RG_BLOCK)
    r = jax.nn.sigmoid(jnp.einsum('btgi,gij->btgj', xb, wa.astype(jnp.float32)).reshape(B, T, D_RG) + ba)
    i = jax.nn.sigmoid(jnp.einsum('btgi,gij->btgj', xb, wx.astype(jnp.float32)).reshape(B, T, D_RG) + bx)
    log_a = -RG_C * r * jax.nn.softplus(-lam.astype(jnp.float32))
    a = jnp.exp(log_a)
    mult = jnp.sqrt(-jnp.expm1(2.0 * log_a))
    mult = jnp.where(jnp.arange(T)[None, :, None] == 0, 1.0, mult)
    u = mult * (i * xf)

    def combine(left, right):
        a_l, b_l = left
        a_r, b_r = right
        return a_l * a_r, a_r * b_l + b_r

    _, h = lax.associative_scan(combine, (a, u), axis=1)
    return h.astype(x.dtype)


def gla_chunked(q, k, logf, v):
    B, T, H, K = q.shape
    V = v.shape[-1]
    C = HG_CHUNK
    N = T // C

    def chunks(t):
        return t.reshape(B, N, C, H, t.shape[-1]).transpose(0, 3, 1, 2, 4)

    q, k, logf, v = chunks(q), chunks(k), chunks(logf), chunks(v)
    b = jnp.cumsum(logf, axis=3)
    b_last = b[:, :, :, -1:, :]
    qe = q * jnp.exp(b)
    ke = k * jnp.exp(-b)
    scores = jnp.einsum('bhnck,bhnsk->bhncs', qe, ke)
    causal = jnp.tril(jnp.ones((C, C), dtype=bool))
    scores = jnp.where(causal, scores, 0.0)
    o_intra = jnp.einsum('bhncs,bhnsv->bhncv', scores, v)

    kd = k * jnp.exp(b_last - b)
    dS = jnp.einsum('bhnsk,bhnsv->bhnkv', kd, v)
    decay = jnp.exp(b_last[:, :, :, 0, :])

    def step(S, inp):
        d, ds = inp
        return d[..., None] * S + ds, S

    S0 = jnp.zeros((B, H, K, V), jnp.float32)
    _, S_prev = lax.scan(step, S0, (jnp.moveaxis(decay, 2, 0), jnp.moveaxis(dS, 2, 0)))
    o_inter = jnp.einsum('bhnck,nbhkv->bhncv', qe, S_prev)
    o = o_intra + o_inter
    return o.transpose(0, 2, 3, 1, 4).reshape(B, T, H, V)


def hgrn2_branch(q, fz, iv, g, lb, norm_w):
    B, T, _ = q.shape
    qf, fzf, ivf, gf = (t.astype(jnp.float32) for t in (q, fz, iv, g))
    lb = lb.astype(jnp.float32)
    f = lb + (1.0 - lb) * jax.nn.sigmoid(fzf)
    logf = jnp.log(f)
    k = (1.0 - lb) * jax.nn.sigmoid(-fzf)
    qs = jax.nn.silu(qf) * (HG_HEAD_DIM ** -0.5)
    heads = lambda t: t.reshape(B, T, HG_HEADS, HG_HEAD_DIM)
    o = gla_chunked(heads(qs), heads(k), heads(logf), heads(ivf))
    o = o * lax.rsqrt(jnp.mean(o * o, axis=-1, keepdims=True) + EPS) * norm_w.astype(jnp.float32)
    o = o.reshape(B, T, D_HG) * jax.nn.silu(gf)
    return o.astype(q.dtype)


def setup_inputs(seed: int = 0) -> dict:
    key = jax.random.key(seed)
    ks = jax.random.split(key, 20)
    f32 = jnp.float32
    nrm = lambda k, shape, scale: scale * jax.random.normal(k, shape, f32)
    u = jax.random.uniform(ks[10], (DEPTH, D_RG), f32, minval=0.9, maxval=0.999)
    s = u ** (1.0 / RG_C)
    rg_lambda = jnp.log(s) - jnp.log1p(-s)
    return {
        "x": jax.random.normal(ks[0], (BATCH, SEQ, D_MODEL), f32),
        "p": jax.random.normal(ks[1], (DEPTH, BATCH, SEQ, PLE_DIM), f32),
        "norm_mix_w": 1.0 + nrm(ks[2], (DEPTH, D_MODEL), 0.1),
        "w_in": nrm(ks[3], (DEPTH, D_MODEL, D_IN), D_MODEL ** -0.5),
        "conv_w": nrm(ks[4], (DEPTH, CONV_WIDTH, D_RG), CONV_WIDTH ** -0.5),
        "conv_b": nrm(ks[5], (DEPTH, D_RG), 0.01),
        "rg_wa": nrm(ks[6], (DEPTH, RG_BLOCKS, RG_BLOCK, RG_BLOCK), RG_BLOCK ** -0.5),
        "rg_ba": nrm(ks[7], (DEPTH, D_RG), 0.1),
        "rg_wx": nrm(ks[8], (DEPTH, RG_BLOCKS, RG_BLOCK, RG_BLOCK), RG_BLOCK ** -0.5),
        "rg_bx": nrm(ks[9], (DEPTH, D_RG), 0.1),
        "rg_lambda": rg_lambda,
        "hg_lb": nrm(ks[11], (DEPTH + 1, D_HG), 0.1),
        "hg_norm_w": 1.0 + nrm(ks[12], (DEPTH, HG_HEAD_DIM), 0.1),
        "w_out": nrm(ks[13], (DEPTH, D_MIX, D_MODEL), D_MIX ** -0.5),
        "ple_norm_w": 1.0 + nrm(ks[14], (DEPTH, D_MODEL), 0.1),
        "w_ple_gate": nrm(ks[15], (DEPTH, D_MODEL, D_MODEL), D_MODEL ** -0.5),
        "b_ple_gate": nrm(ks[16], (DEPTH, D_MODEL), 0.1),
        "w_ple_proj": nrm(ks[17], (DEPTH, PLE_DIM, D_MODEL), PLE_DIM ** -0.5),
        "final_norm_w": 1.0 + nrm(ks[18], (D_MODEL,), 0.1),
    }


def reference(x, p, norm_mix_w, w_in, conv_w, conv_b, rg_wa, rg_ba, rg_wx, rg_bx,
              rg_lambda, hg_lb, hg_norm_w, w_out, ple_norm_w, w_ple_gate, b_ple_gate,
              w_ple_proj, final_norm_w):
    lb_all = jnp.cumsum(jax.nn.softmax(hg_lb.astype(jnp.float32), axis=0), axis=0)
    split_at = [D_RG, 2 * D_RG, 2 * D_RG + D_HG, 2 * D_RG + 2 * D_HG, 2 * D_RG + 3 * D_HG]
    h = x
    for l in range(DEPTH):
        u = rms_norm(h, norm_mix_w[l])
        proj = u @ w_in[l]
        xa, ga, qb, fb, ib, gb = jnp.split(proj, split_at, axis=-1)
        xa = causal_depthwise_conv(xa, conv_w[l], conv_b[l])
        ya = rg_lru(xa, rg_wa[l], rg_ba[l], rg_wx[l], rg_bx[l], rg_lambda[l]) * jax.nn.silu(ga)
        yb = hgrn2_branch(qb, fb, ib, gb, lb_all[l], hg_norm_w[l])
        h = h + jnp.concatenate([ya, yb], axis=-1) @ w_out[l]
        gate = jax.nn.sigmoid(rms_norm(h, ple_norm_w[l]) @ w_ple_gate[l] + b_ple_gate[l])
        h = h + gate * (p[l] @ w_ple_proj[l])
    return rms_norm(h, final_norm_w)
```

```python
import functools

import jax
import jax.numpy as jnp
from jax import lax
from jax.experimental import pallas as pl
from jax.experimental.pallas import tpu as pltpu

D_MODEL = 1024
PLE_DIM = 256
D_RG = 512
RG_BLOCKS = 8
RG_BLOCK = 64
CONV_WIDTH = 4
RG_C = 8.0
D_HG = 512
HG_HEAD_DIM = 128
HG_HEADS = 4
HG_CHUNK = 64
D_IN = 2 * D_RG + 4 * D_HG
EPS = 1e-6

SUBLANES = 8
RG_GROUP = 256
TOKEN_BLOCK = 512
VMEM_LIMIT_BYTES = 56 * 1024 * 1024

BF16 = jnp.bfloat16
F32 = jnp.float32


def _dot(a, b):
    return jnp.dot(a, b, preferred_element_type=F32)


def _dot_nt(a, b):
    return lax.dot_general(a, b, (((1,), (1,)), ((), ())), preferred_element_type=F32)


def _dot_tn(a, b):
    return lax.dot_general(a, b, (((0,), (0,)), ((), ())), preferred_element_type=F32)


def _rms_norm(x, w):
    return x * lax.rsqrt(jnp.mean(x * x, axis=-1, keepdims=True) + EPS) * w


def _sigmoid(x):
    return 1.0 / (1.0 + jnp.exp(-x))


def _block_kernel(
    x_ref, p_ref, nmix_ref, w_in_ref, conv_w_ref, conv_b_ref, wa_ref, ba_ref, wx_ref, bx_ref,
    lam_ref, hg_lb_ref, hg_nw_ref, w_out_ref, ple_nw_ref, w_gate_ref, b_gate_ref, w_ple_ref,
    fin_nw_ref, tril_ref,
    o_ref,
    proj_buf, xa_ext, a_buf, u_buf, y_buf, h_carry, st_buf,
):
    tb = x_ref.shape[0]
    t_idx = pl.program_id(1)
    first = t_idx == 0

    x = x_ref[...]
    un = _rms_norm(x, nmix_ref[...]).astype(BF16)
    proj_buf[...] = _dot(un, w_in_ref[...])

    @pl.when(first)
    def _():
        xa_ext[0:SUBLANES, :] = jnp.zeros((SUBLANES, D_RG), F32)
        h_carry[...] = jnp.zeros_like(h_carry)
        st_buf[...] = jnp.zeros_like(st_buf)

    @pl.when(jnp.logical_not(first))
    def _():
        xa_ext[0:SUBLANES, :] = xa_ext[tb:tb + SUBLANES, :]

    xa_ext[SUBLANES:SUBLANES + tb, :] = proj_buf[:, 0:D_RG]
    xc = conv_b_ref[...]
    for j in range(CONV_WIDTH):
        off = SUBLANES - (CONV_WIDTH - 1) + j
        xc = xc + xa_ext[off:off + tb, :] * conv_w_ref[j:j + 1, :]

    xc_b = xc.astype(BF16)
    r_parts, i_parts = [], []
    for g in range(D_RG // RG_GROUP):
        xg = xc_b[:, g * RG_GROUP:(g + 1) * RG_GROUP]
        r_parts.append(_dot(xg, wa_ref[g]))
        i_parts.append(_dot(xg, wx_ref[g]))
    r = _sigmoid(jnp.concatenate(r_parts, axis=-1) + ba_ref[...])
    ig = _sigmoid(jnp.concatenate(i_parts, axis=-1) + bx_ref[...])

    z = -lam_ref[...]
    softplus = jnp.maximum(z, 0.0) + jnp.log1p(jnp.exp(-jnp.abs(z)))
    log_a = (-RG_C) * r * softplus
    a = jnp.exp(log_a)
    mult = jnp.sqrt(-jnp.tanh(log_a) * (a * a + 1.0))
    row = lax.broadcasted_iota(jnp.int32, (tb, 1), 0)
    mult = jnp.where(jnp.logical_and(first, row == 0), 1.0, mult)
    a_buf[...] = a
    u_buf[...] = mult * (ig * xc)

    srow = lax.broadcasted_iota(jnp.int32, (SUBLANES, D_RG), 0)

    def scan_body(i, hprev):
        r0 = pl.multiple_of(i * SUBLANES, SUBLANES)
        av = a_buf[pl.ds(r0, SUBLANES), :]
        uv = u_buf[pl.ds(r0, SUBLANES), :]
        for d in (1, 2, 4):
            keep = srow >= d
            a_s = pltpu.roll(av, d, axis=0)
            u_s = pltpu.roll(uv, d, axis=0)
            uv = jnp.where(keep, av * u_s + uv, uv)
            av = jnp.where(keep, av * a_s, av)
        h = uv + av * hprev
        u_buf[pl.ds(r0, SUBLANES), :] = h
        return h[SUBLANES - 1:SUBLANES, :]

    h_last = lax.fori_loop(0, tb // SUBLANES, scan_body, h_carry[0:1, :], unroll=8)
    h_carry[...] = jnp.broadcast_to(h_last, h_carry.shape)

    ga = proj_buf[:, D_RG:2 * D_RG]
    y_buf[:, 0:D_RG] = (u_buf[...] * (ga * _sigmoid(ga))).astype(BF16)

    hg = hg_lb_ref[...]
    hg_max = jnp.max(hg, axis=0, keepdims=True)
    hg_e = jnp.exp(hg - hg_max)
    lb = hg_e[0:1, :] / jnp.sum(hg_e, axis=0, keepdims=True)
    one_m_lb = 1.0 - lb
    q_off, f_off, i_off, g_off = 2 * D_RG, 2 * D_RG + D_HG, 2 * D_RG + 2 * D_HG, 2 * D_RG + 3 * D_HG
    tril = tril_ref[...]
    crow = lax.broadcasted_iota(jnp.int32, (HG_CHUNK, HG_CHUNK), 0)
    ccol = lax.broadcasted_iota(jnp.int32, (HG_CHUNK, HG_CHUNK), 1)
    causal = crow >= ccol
    q_scale = HG_HEAD_DIM ** -0.5

    def chunk_body(c, carry):
        r0 = pl.multiple_of(c * HG_CHUNK, HG_CHUNK)
        rows = pl.ds(r0, HG_CHUNK)
        fz = proj_buf[rows, f_off:f_off + D_HG]
        sig = _sigmoid(fz)
        logf = jnp.log(lb + one_m_lb * sig)
        k = one_m_lb * (1.0 - sig)
        l_hi = logf.astype(BF16)
        rem = logf - l_hi.astype(F32)
        l_mid = rem.astype(BF16)
        l_lo = (rem - l_mid.astype(F32)).astype(BF16)
        b = _dot(tril, l_hi) + _dot(tril, l_mid) + _dot(tril, l_lo)
        b_last = b[HG_CHUNK - 1:HG_CHUNK, :]
        qz = proj_buf[rows, q_off:q_off + D_HG]
        qs = qz * _sigmoid(qz) * q_scale
        qe = (qs * jnp.exp(b)).astype(BF16)
        ke = (k * jnp.exp(-b)).astype(BF16)
        kd = (k * jnp.exp(b_last - b)).astype(BF16)
        decay = jnp.exp(b_last)
        v = proj_buf[rows, i_off:i_off + D_HG].astype(BF16)
        gz = proj_buf[rows, g_off:g_off + D_HG]
        gate = gz * _sigmoid(gz)
        outs = []
        for h in range(HG_HEADS):
            ls = slice(h * HG_HEAD_DIM, (h + 1) * HG_HEAD_DIM)
            qe_h, ke_h, kd_h, v_h = qe[:, ls], ke[:, ls], kd[:, ls], v[:, ls]
            st = st_buf[h]
            sc = jnp.where(causal, _dot_nt(qe_h, ke_h), 0.0).astype(BF16)
            o = _dot(sc, v_h) + _dot_nt(qe_h, st.astype(BF16))
            st_buf[h] = st * decay[:, ls] + _dot_tn(v_h, kd_h)
            o = o * lax.rsqrt(jnp.mean(o * o, axis=-1, keepdims=True) + EPS) * hg_nw_ref[...]
            outs.append(o)
        yb = jnp.concatenate(outs, axis=-1) * gate
        y_buf[rows, D_RG:D_RG + D_HG] = yb.astype(BF16)
        return carry

    lax.fori_loop(0, tb // HG_CHUNK, chunk_body, 0)

    h1 = x_ref[...] + _dot(y_buf[...], w_out_ref[...])
    n2 = _rms_norm(h1, ple_nw_ref[...]).astype(BF16)
    gate = _sigmoid(_dot(n2, w_gate_ref[...]) + b_gate_ref[...])
    pp = _dot(p_ref[...].astype(BF16), w_ple_ref[...])
    h2 = h1 + gate * pp
    o_ref[...] = _rms_norm(h2, fin_nw_ref[...])


def _block_diag(w):
    n_grp = D_RG // RG_GROUP
    per = RG_GROUP // RG_BLOCK
    w = w.reshape(n_grp, per, RG_BLOCK, RG_BLOCK)
    eye = jnp.eye(per, dtype=w.dtype)
    bd = w[:, :, :, None, :] * eye[None, :, None, :, None]
    return bd.reshape(n_grp, RG_GROUP, RG_GROUP)


def _const_spec(shape):
    zeros = (0,) * len(shape)
    return pl.BlockSpec(shape, lambda b, t: zeros, pipeline_mode=pl.Buffered(1))


@jax.jit
def kernel(x, p, norm_mix_w, w_in, conv_w, conv_b, rg_wa, rg_ba, rg_wx, rg_bx, rg_lambda, hg_lb,
           hg_norm_w, w_out, ple_norm_w, w_ple_gate, b_ple_gate, w_ple_proj, final_norm_w):
    batch, seq, d_model = x.shape
    depth = w_in.shape[0]
    assert depth == 1 and d_model == D_MODEL and seq % TOKEN_BLOCK == 0
    tb = TOKEN_BLOCK
    nt = seq // tb
    l = 0

    x2 = x.reshape(batch * seq, d_model)
    p2 = p[l].reshape(batch * seq, PLE_DIM)
    row2 = lambda v: v.reshape(1, -1).astype(F32)
    tril = jnp.tril(jnp.ones((HG_CHUNK, HG_CHUNK), F32)).astype(BF16)

    operands = [
        x2, p2, row2(norm_mix_w[l]), w_in[l].astype(BF16), conv_w[l].astype(F32), row2(conv_b[l]),
        _block_diag(rg_wa[l]).astype(BF16), row2(rg_ba[l]), _block_diag(rg_wx[l]).astype(BF16),
        row2(rg_bx[l]), row2(rg_lambda[l]), hg_lb.astype(F32), row2(hg_norm_w[l]),
        w_out[l].astype(BF16), row2(ple_norm_w[l]), w_ple_gate[l].astype(BF16), row2(b_ple_gate[l]),
        w_ple_proj[l].astype(BF16), row2(final_norm_w), tril,
    ]
    tok_map = lambda b, t: (b * nt + t, 0)
    in_specs = [pl.BlockSpec((tb, d_model), tok_map), pl.BlockSpec((tb, PLE_DIM), tok_map)]
    in_specs += [_const_spec(op.shape) for op in operands[2:]]

    out = pl.pallas_call(
        _block_kernel,
        grid=(batch, nt),
        in_specs=in_specs,
        out_specs=pl.BlockSpec((tb, d_model), tok_map),
        out_shape=jax.ShapeDtypeStruct((batch * seq, d_model), x.dtype),
        scratch_shapes=[
            pltpu.VMEM((tb, D_IN), F32),
            pltpu.VMEM((tb + SUBLANES, D_RG), F32),
            pltpu.VMEM((tb, D_RG), F32),
            pltpu.VMEM((tb, D_RG), F32),
            pltpu.VMEM((tb, D_RG + D_HG), BF16),
            pltpu.VMEM((SUBLANES, D_RG), F32),
            pltpu.VMEM((HG_HEADS, HG_HEAD_DIM, HG_HEAD_DIM), F32),
        ],
        compiler_params=pltpu.CompilerParams(
            dimension_semantics=("arbitrary", "arbitrary"),
            vmem_limit_bytes=VMEM_LIMIT_BYTES,
        ),
        name="hymba_block",
    )(*operands)
    return out.reshape(batch, seq, d_model)
```

```python
import functools

import jax
import jax.numpy as jnp
from jax import lax
from jax.experimental import pallas as pl
from jax.experimental.pallas import tpu as pltpu

D_MODEL = 1024
PLE_DIM = 256
D_RG = 512
RG_BLOCKS = 8
RG_BLOCK = 64
CONV_WIDTH = 4
RG_C = 8.0
D_HG = 512
HG_HEAD_DIM = 128
HG_HEADS = 4
HG_CHUNK = 64
D_IN = 2 * D_RG + 4 * D_HG
EPS = 1e-6

SUBLANES = 8
RG_GROUP = 256
TOKEN_BLOCK = 512
VMEM_LIMIT_BYTES = 56 * 1024 * 1024

BF16 = jnp.bfloat16
F32 = jnp.float32


def _dot(a, b):
    return jnp.dot(a, b, preferred_element_type=F32)


def _dot_nt(a, b):
    return lax.dot_general(a, b, (((1,), (1,)), ((), ())), preferred_element_type=F32)


def _dot_tn(a, b):
    return lax.dot_general(a, b, (((0,), (0,)), ((), ())), preferred_element_type=F32)


def _rms_norm(x, w):
    return x * lax.rsqrt(jnp.mean(x * x, axis=-1, keepdims=True) + EPS) * w


def _sigmoid(x):
    return 1.0 / (1.0 + jnp.exp(-x))


def _block_kernel(
    xin_ref, x_ref, p_ref, nmix_ref, w_in_ref, conv_w_ref, conv_b_ref, wa_ref, ba_ref, wx_ref, bx_ref,
    lam_ref, hg_lb_ref, hg_nw_ref, w_out_ref, ple_nw_ref, w_gate_ref, b_gate_ref, w_ple_ref,
    fin_nw_ref, tril_ref,
    o_ref,
    proj_buf, xa_ext, a_buf, u_buf, y_buf, h_carry, st_buf, proj_stage,
    *, nt,
):
    tb = x_ref.shape[0]
    step = pl.program_id(0)
    first = lax.rem(jnp.maximum(step - 1, 0), nt) == 0

    @pl.when(step == 0)
    def _():
        proj_stage[...] = jnp.zeros_like(proj_stage)

    @pl.when(first)
    def _():
        xa_ext[0:SUBLANES, :] = jnp.zeros((SUBLANES, D_RG), F32)
        h_carry[...] = jnp.zeros_like(h_carry)
        st_buf[...] = jnp.zeros_like(st_buf)

    @pl.when(jnp.logical_not(first))
    def _():
        xa_ext[0:SUBLANES, :] = xa_ext[tb:tb + SUBLANES, :]

    proj_buf[...] = proj_stage[...]
    un = _rms_norm(xin_ref[...], nmix_ref[...]).astype(BF16)
    n_slab = 256
    slabs = list(range(D_IN // n_slab))

    def emit_slab():
        s = slabs.pop(0)
        cols = slice(s * n_slab, (s + 1) * n_slab)
        proj_stage[:, cols] = _dot(un, w_in_ref[:, cols])

    emit_slab()
    xa_ext[SUBLANES:SUBLANES + tb, :] = proj_buf[:, 0:D_RG]
    xc = conv_b_ref[...]
    for j in range(CONV_WIDTH):
        off = SUBLANES - (CONV_WIDTH - 1) + j
        xc = xc + xa_ext[off:off + tb, :] * conv_w_ref[j:j + 1, :]

    emit_slab()
    xc_b = xc.astype(BF16)
    r_parts, i_parts = [], []
    for g in range(D_RG // RG_GROUP):
        xg = xc_b[:, g * RG_GROUP:(g + 1) * RG_GROUP]
        r_parts.append(_dot(xg, wa_ref[g]))
        i_parts.append(_dot(xg, wx_ref[g]))
    r = _sigmoid(jnp.concatenate(r_parts, axis=-1) + ba_ref[...])
    ig = _sigmoid(jnp.concatenate(i_parts, axis=-1) + bx_ref[...])

    z = -lam_ref[...]
    softplus = jnp.maximum(z, 0.0) + jnp.log1p(jnp.exp(-jnp.abs(z)))
    log_a = (-RG_C) * r * softplus
    a = jnp.exp(log_a)
    mult = jnp.sqrt(-jnp.tanh(log_a) * (a * a + 1.0))
    row = lax.broadcasted_iota(jnp.int32, (tb, 1), 0)
    mult = jnp.where(jnp.logical_and(first, row == 0), 1.0, mult)
    emit_slab()
    a_buf[...] = a
    u_buf[...] = mult * (ig * xc)

    srow = lax.broadcasted_iota(jnp.int32, (SUBLANES, D_RG), 0)

    def scan_body(i, hprev):
        r0 = pl.multiple_of(i * SUBLANES, SUBLANES)
        av = a_buf[pl.ds(r0, SUBLANES), :]
        uv = u_buf[pl.ds(r0, SUBLANES), :]
        for d in (1, 2, 4):
            keep = srow >= d
            a_s = pltpu.roll(av, d, axis=0)
            u_s = pltpu.roll(uv, d, axis=0)
            uv = jnp.where(keep, av * u_s + uv, uv)
            av = jnp.where(keep, av * a_s, av)
        h = uv + av * hprev
        u_buf[pl.ds(r0, SUBLANES), :] = h
        return h[SUBLANES - 1:SUBLANES, :]

    h_last = lax.fori_loop(0, tb // SUBLANES, scan_body, h_carry[0:1, :], unroll=True)
    h_carry[...] = jnp.broadcast_to(h_last, h_carry.shape)

    emit_slab()
    ga = proj_buf[:, D_RG:2 * D_RG]
    y_buf[:, 0:D_RG] = (u_buf[...] * (ga * _sigmoid(ga))).astype(BF16)

    hg = hg_lb_ref[...]
    hg_max = jnp.max(hg, axis=0, keepdims=True)
    hg_e = jnp.exp(hg - hg_max)
    lb = hg_e[0:1, :] / jnp.sum(hg_e, axis=0, keepdims=True)
    one_m_lb = 1.0 - lb
    q_off, f_off, i_off, g_off = 2 * D_RG, 2 * D_RG + D_HG, 2 * D_RG + 2 * D_HG, 2 * D_RG + 3 * D_HG
    tril = tril_ref[...]
    crow = lax.broadcasted_iota(jnp.int32, (HG_CHUNK, HG_CHUNK), 0)
    ccol = lax.broadcasted_iota(jnp.int32, (HG_CHUNK, HG_CHUNK), 1)
    causal = crow >= ccol
    q_scale = HG_HEAD_DIM ** -0.5

    def chunk_body(c, carry):
        emit_slab()
        r0 = c * HG_CHUNK
        rows = pl.ds(r0, HG_CHUNK)
        fz = proj_buf[rows, f_off:f_off + D_HG]
        sig = _sigmoid(fz)
        logf = jnp.log(lb + one_m_lb * sig)
        k = one_m_lb * (1.0 - sig)
        l_hi = logf.astype(BF16)
        rem = logf - l_hi.astype(F32)
        l_mid = rem.astype(BF16)
        l_lo = (rem - l_mid.astype(F32)).astype(BF16)
        b = _dot(tril, l_hi) + _dot(tril, l_mid) + _dot(tril, l_lo)
        b_last = b[HG_CHUNK - 1:HG_CHUNK, :]
        qz = proj_buf[rows, q_off:q_off + D_HG]
        qs = qz * _sigmoid(qz) * q_scale
        qe = (qs * jnp.exp(b)).astype(BF16)
        ke = (k * jnp.exp(-b)).astype(BF16)
        kd = (k * jnp.exp(b_last - b)).astype(BF16)
        decay = jnp.exp(b_last)
        v = proj_buf[rows, i_off:i_off + D_HG].astype(BF16)
        gz = proj_buf[rows, g_off:g_off + D_HG]
        gate = gz * _sigmoid(gz)
        outs = []
        for h in range(HG_HEADS):
            ls = slice(h * HG_HEAD_DIM, (h + 1) * HG_HEAD_DIM)
            qe_h, ke_h, kd_h, v_h = qe[:, ls], ke[:, ls], kd[:, ls], v[:, ls]
            st = st_buf[h]
            sc = jnp.where(causal, _dot_nt(qe_h, ke_h), 0.0).astype(BF16)
            o = _dot(sc, v_h) + _dot_nt(qe_h, st.astype(BF16))
            st_buf[h] = st * decay[:, ls] + _dot_tn(v_h, kd_h)
            o = o * lax.rsqrt(jnp.mean(o * o, axis=-1, keepdims=True) + EPS) * hg_nw_ref[...]
            outs.append(o)
        yb = jnp.concatenate(outs, axis=-1) * gate
        y_buf[rows, D_RG:D_RG + D_HG] = yb.astype(BF16)
        return carry

    for c in range(tb // HG_CHUNK):
        chunk_body(c, 0)
    assert not slabs

    h1 = x_ref[...] + _dot(y_buf[...], w_out_ref[...])
    n2 = _rms_norm(h1, ple_nw_ref[...]).astype(BF16)
    gate = _sigmoid(_dot(n2, w_gate_ref[...]) + b_gate_ref[...])
    pp = _dot(p_ref[...].astype(BF16), w_ple_ref[...])
    h2 = h1 + gate * pp
    o_ref[...] = _rms_norm(h2, fin_nw_ref[...])


def _block_diag(w):
    n_grp = D_RG // RG_GROUP
    per = RG_GROUP // RG_BLOCK
    w = w.reshape(n_grp, per, RG_BLOCK, RG_BLOCK)
    eye = jnp.eye(per, dtype=w.dtype)
    bd = w[:, :, :, None, :] * eye[None, :, None, :, None]
    return bd.reshape(n_grp, RG_GROUP, RG_GROUP)


def _const_spec(shape):
    zeros = (0,) * len(shape)
    return pl.BlockSpec(shape, lambda i: zeros, pipeline_mode=pl.Buffered(1))


@jax.jit
def kernel(x, p, norm_mix_w, w_in, conv_w, conv_b, rg_wa, rg_ba, rg_wx, rg_bx, rg_lambda, hg_lb,
           hg_norm_w, w_out, ple_norm_w, w_ple_gate, b_ple_gate, w_ple_proj, final_norm_w):
    batch, seq, d_model = x.shape
    depth = w_in.shape[0]
    assert depth == 1 and d_model == D_MODEL and seq % TOKEN_BLOCK == 0
    tb = TOKEN_BLOCK
    nt = seq // tb
    l = 0

    nblk = batch * nt
    x2 = x.reshape(batch * seq, d_model)
    p2 = p[l].reshape(batch * seq, PLE_DIM)
    row2 = lambda v: v.reshape(1, -1).astype(F32)
    tril = jnp.tril(jnp.ones((HG_CHUNK, HG_CHUNK), F32)).astype(BF16)

    operands = [
        x2, x2, p2, row2(norm_mix_w[l]), w_in[l].astype(BF16), conv_w[l].astype(F32), row2(conv_b[l]),
        _block_diag(rg_wa[l]).astype(BF16), row2(rg_ba[l]), _block_diag(rg_wx[l]).astype(BF16),
        row2(rg_bx[l]), row2(rg_lambda[l]), hg_lb.astype(F32), row2(hg_norm_w[l]),
        w_out[l].astype(BF16), row2(ple_norm_w[l]), w_ple_gate[l].astype(BF16), row2(b_ple_gate[l]),
        w_ple_proj[l].astype(BF16), row2(final_norm_w), tril,
    ]
    tok_map = lambda i: (jnp.maximum(i - 1, 0), 0)
    nxt_map = lambda i: (jnp.minimum(i, nblk - 1), 0)
    in_specs = [pl.BlockSpec((tb, d_model), nxt_map), pl.BlockSpec((tb, d_model), tok_map),
                pl.BlockSpec((tb, PLE_DIM), tok_map)]
    in_specs += [_const_spec(op.shape) for op in operands[3:]]

    out = pl.pallas_call(
        functools.partial(_block_kernel, nt=nt),
        grid=(nblk + 1,),
        in_specs=in_specs,
        out_specs=pl.BlockSpec((tb, d_model), tok_map),
        out_shape=jax.ShapeDtypeStruct((batch * seq, d_model), x.dtype),
        scratch_shapes=[
            pltpu.VMEM((tb, D_IN), F32),
            pltpu.VMEM((tb + SUBLANES, D_RG), F32),
            pltpu.VMEM((tb, D_RG), F32),
            pltpu.VMEM((tb, D_RG), F32),
            pltpu.VMEM((tb, D_RG + D_HG), BF16),
            pltpu.VMEM((SUBLANES, D_RG), F32),
            pltpu.VMEM((HG_HEADS, HG_HEAD_DIM, HG_HEAD_DIM), F32),
            pltpu.VMEM((tb, D_IN), F32),
        ],
        compiler_params=pltpu.CompilerParams(
            dimension_semantics=("arbitrary",),
            vmem_limit_bytes=VMEM_LIMIT_BYTES,
        ),
        name="hymba_block",
    )(*operands)
    return out.reshape(batch, seq, d_model)
```
